```python
import math
import jax, jax.numpy as jnp
from jax import lax
import numpy as np

D_MODEL = 1024
BATCH = 4
SEQ = 4096
DEPTH = 2

CHUNK = 64
HEAD_DIM = 64
RET_HEADS = 6
SB_HEADS = 6
GLA_HEADS = 4
RET_W = RET_HEADS * HEAD_DIM
SB_W = SB_HEADS * HEAD_DIM
GLA_W = GLA_HEADS * HEAD_DIM
GLA_GATE_RANK = 16
GLA_TAU = 16.0
D_FF = 2816
FFN_CONV = 3
SB_BLOCK = 128
ROPE_BASE = 10000.0
EPS = 1e-6
IN_COLS = 4 * RET_W + 3 * SB_W + 4 * GLA_W + GLA_GATE_RANK

kernel_name = "hybrid_retention_stickbreak_gla_convffn"


def rmsnorm(x, g):
    xf = x.astype(jnp.float32)
    y = xf * lax.rsqrt(jnp.mean(xf * xf, axis=-1, keepdims=True) + EPS)
    return (y * g.astype(jnp.float32)).astype(x.dtype)


def split_heads(t, n_heads):
    b, s, _ = t.shape
    return t.reshape(b, s, n_heads, HEAD_DIM).transpose(0, 2, 1, 3)


def merge_heads(t):
    b, h, s, d = t.shape
    return t.transpose(0, 2, 1, 3).reshape(b, s, h * d)


def rotary(t):
    s, d = t.shape[2], t.shape[3]
    half = d // 2
    pos = jnp.arange(s, dtype=jnp.float32)
    freqs = ROPE_BASE ** (-jnp.arange(half, dtype=jnp.float32) / half)
    ang = pos[:, None] * freqs[None, :]
    cos, sin = jnp.cos(ang), jnp.sin(ang)
    tf = t.astype(jnp.float32)
    t1, t2 = tf[..., :half], tf[..., half:]
    return jnp.concatenate([t1 * cos - t2 * sin, t1 * sin + t2 * cos], axis=-1)


def head_groupnorm(o, g):
    h, d = o.shape[1], o.shape[3]
    mu = jnp.mean(o, axis=-1, keepdims=True)
    var = jnp.mean(jnp.square(o - mu), axis=-1, keepdims=True)
    return (o - mu) * lax.rsqrt(var + EPS) * g.astype(jnp.float32).reshape(h, 1, d)


def head_rmsnorm(o, g):
    h, d = o.shape[1], o.shape[3]
    return o * lax.rsqrt(jnp.mean(o * o, axis=-1, keepdims=True) + EPS) * g.astype(jnp.float32).reshape(h, 1, d)


def retention(q, k, v):
    b, h, s, d = q.shape
    n = s // CHUNK
    log_gamma = jnp.log1p(-(2.0 ** (-5.0 - jnp.arange(h, dtype=jnp.float32))))
    qc = q.reshape(b, h, n, CHUNK, d)
    kc = k.reshape(b, h, n, CHUNK, d) * (d ** -0.5)
    vc = v.reshape(b, h, n, CHUNK, d)
    pos = jnp.arange(CHUNK, dtype=jnp.float32)
    rel = jnp.abs(pos[:, None] - pos[None, :])
    d_intra = jnp.exp(log_gamma[:, None, None] * rel)
    scores = jnp.einsum('bhncd,bhnsd->bhncs', qc, kc) * d_intra[None, :, None]
    intra = jnp.einsum('bhncs,bhnse->bhnce', scores, vc)
    k_decay = jnp.exp(log_gamma[:, None] * (CHUNK - 1 - pos))
    kv = jnp.einsum('bhnsd,bhnse->bhnde', kc * k_decay[None, :, None, :, None], vc)
    chunk_decay = jnp.exp(log_gamma * CHUNK)[None, :, None, None]

    def step(state, kv_i):
        return chunk_decay * state + kv_i, state

    _, prev = lax.scan(step, jnp.zeros((b, h, d, d), jnp.float32), jnp.moveaxis(kv, 2, 0))
    prev = jnp.moveaxis(prev, 0, 2)
    q_decay = jnp.exp(log_gamma[:, None] * (pos + 1.0))
    inter = jnp.einsum('bhncd,bhnde->bhnce', qc * q_decay[None, :, None, :, None], prev)
    return (intra + inter).reshape(b, h, s, d)


def stick_breaking(q, k, v):
    b, h, s, d = q.shape
    scale = d ** -0.5
    outs = []
    for blk in range(s // SB_BLOCK):
        q0 = blk * SB_BLOCK
        end = q0 + SB_BLOCK
        z = jnp.einsum('bhtd,bhsd->bhts', q[:, :, q0:end], k[:, :, :end]) * scale
        t_idx = q0 + jnp.arange(SB_BLOCK)
        s_idx = jnp.arange(end)
        strict = s_idx[None, :] < t_idx[:, None]
        log_keep = jnp.where(strict, jax.nn.log_sigmoid(-z), 0.0)
        between = lax.cumsum(log_keep, axis=3, reverse=True) - log_keep
        a = jnp.where(strict, jnp.exp(jax.nn.log_sigmoid(z) + between), 0.0)
        outs.append(jnp.einsum('bhts,bhse->bhte', a, v[:, :, :end]))
    return jnp.concatenate(outs, axis=2)


def gla(q, k, v, log_alpha):
    b, h, s, d = q.shape
    n = s // CHUNK
    qc = q.reshape(b, h, n, CHUNK, d) * (d ** -0.5)
    kc = k.reshape(b, h, n, CHUNK, d)
    vc = v.reshape(b, h, n, CHUNK, d)
    bcum = jnp.cumsum(log_alpha.reshape(b, h, n, CHUNK, d), axis=3)
    blast = bcum[:, :, :, -1]
    kv = jnp.einsum('bhnsd,bhnse->bhnde', kc * jnp.exp(blast[:, :, :, None] - bcum), vc)

    def step(state, inp):
        q_i, k_i, v_i, b_i, kv_i, bl_i = inp
        inter = jnp.einsum('bhcd,bhde->bhce', q_i * jnp.exp(b_i), state)
        w = jnp.exp(-jnp.abs(b_i[:, :, :, None, :] - b_i[:, :, None, :, :]))
        scores = jnp.einsum('bhtd,bhsd,bhtsd->bhts', q_i, k_i, w)
        intra = jnp.einsum('bhts,bhse->bhte', scores, v_i)
        new_state = jnp.exp(bl_i)[..., None] * state + kv_i
        return new_state, inter + intra

    xs = tuple(jnp.moveaxis(t, 2, 0) for t in (qc, kc, vc, bcum, kv, blast))
    _, out = lax.scan(step, jnp.zeros((b, h, d, d), jnp.float32), xs)
    return jnp.moveaxis(out, 0, 2).reshape(b, h, s, d)


def token_mixer(hn, w_in, gla_w2, gla_b, ret_norm_g, gla_norm_g, w_out):
    widths = [RET_W] * 4 + [SB_W] * 3 + [GLA_W] * 4 + [GLA_GATE_RANK]
    idx = []
    acc = 0
    for wd in widths[:-1]:
        acc += wd
        idx.append(acc)
    proj = (hn @ w_in).astype(jnp.float32)
    rq, rk, rv, rg, sq, sk, sv, gq, gk, gv, gg, glr = jnp.split(proj, idx, axis=-1)
    ret = retention(rotary(split_heads(rq, RET_HEADS)), rotary(split_heads(rk, RET_HEADS)),
                    split_heads(rv, RET_HEADS))
    ret = merge_heads(head_groupnorm(ret, ret_norm_g)) * jax.nn.silu(rg)
    sb = merge_heads(stick_breaking(split_heads(sq, SB_HEADS), split_heads(sk, SB_HEADS),
                                    split_heads(sv, SB_HEADS)))
    log_alpha = jax.nn.log_sigmoid(glr @ gla_w2.astype(jnp.float32) + gla_b.astype(jnp.float32)) / GLA_TAU
    go = gla(split_heads(gq, GLA_HEADS), split_heads(gk, GLA_HEADS), split_heads(gv, GLA_HEADS),
             split_heads(log_alpha, GLA_HEADS))
    go = merge_heads(head_rmsnorm(go, gla_norm_g)) * jax.nn.silu(gg)
    mixed = jnp.concatenate([ret, sb, go], axis=-1).astype(hn.dtype)
    return mixed @ w_out


def conv_ffn(hn, w_up, conv_w, conv_b, w_down):
    s = hn.shape[1]
    up = hn @ w_up
    a, val = jnp.split(up, 2, axis=-1)
    ap = jnp.pad(a, ((0, 0), (FFN_CONV - 1, 0), (0, 0)))
    a = sum(ap[:, i:i + s] * conv_w[i] for i in range(FFN_CONV)) + conv_b
    return (jax.nn.gelu(a) * val) @ w_down


def setup_inputs(seed: int = 0) -> dict:
    key = jax.random.key(seed)
    ks = jax.random.split(key, 20)
    f32 = jnp.float32
    nrm = lambda k, shape, sc: jax.random.normal(k, shape, f32) * sc
    L, D = DEPTH, D_MODEL
    return {
        "x": nrm(ks[0], (BATCH, SEQ, D), 1.0),
        "c": nrm(ks[1], (BATCH, D), 1.0),
        "ada_w": nrm(ks[2], (L, D, 6 * D), 0.5 * D ** -0.5),
        "ada_b": nrm(ks[3], (L, 6 * D), 0.02),
        "pre_mix_g": 1.0 + nrm(ks[4], (L, D), 0.05),
        "post_mix_g": 1.0 + nrm(ks[5], (L, D), 0.05),
        "w_in": nrm(ks[6], (L, D, IN_COLS), D ** -0.5),
        "gla_w2": nrm(ks[7], (L, GLA_GATE_RANK, GLA_W), GLA_GATE_RANK ** -0.5),
        "gla_b": nrm(ks[8], (L, GLA_W), 0.1),
        "ret_norm_g": 1.0 + nrm(ks[9], (L, RET_W), 0.05),
        "gla_norm_g": 1.0 + nrm(ks[10], (L, GLA_W), 0.05),
        "w_out": nrm(ks[11], (L, D, D), D ** -0.5),
        "pre_ffn_g": 1.0 + nrm(ks[12], (L, D), 0.05),
        "post_ffn_g": 1.0 + nrm(ks[13], (L, D), 0.05),
        "w_up": nrm(ks[14], (L, D, 2 * D_FF), D ** -0.5),
        "conv_w": nrm(ks[15], (L, FFN_CONV, D_FF), FFN_CONV ** -0.5),
        "conv_b": nrm(ks[16], (L, D_FF), 0.02),
        "w_down": nrm(ks[17], (L, D_FF, D), D_FF ** -0.5),
    }


def reference(x, c, ada_w, ada_b, pre_mix_g, post_mix_g, w_in, gla_w2, gla_b, ret_norm_g,
              gla_norm_g, w_out, pre_ffn_g, post_ffn_g, w_up, conv_w, conv_b, w_down):
    for l in range(DEPTH):
        mod = jax.nn.silu(c) @ ada_w[l] + ada_b[l]
        sh1, sc1, g1, sh2, sc2, g2 = jnp.split(mod[:, None, :], 6, axis=-1)
        hn = rmsnorm(x, pre_mix_g[l]) * (1.0 + sc1) + sh1
        y = token_mixer(hn, w_in[l], gla_w2[l], gla_b[l], ret_norm_g[l], gla_norm_g[l], w_out[l])
        x = x + g1 * rmsnorm(y, post_mix_g[l])
        hn = rmsnorm(x, pre_ffn_g[l]) * (1.0 + sc2) + sh2
        y = conv_ffn(hn, w_up[l], conv_w[l], conv_b[l], w_down[l])
        x = x + g2 * rmsnorm(y, post_ffn_g[l])
    return x
```

```python
import functools

import jax
import jax.numpy as jnp
from jax import lax
from jax.experimental import pallas as pl
from jax.experimental.pallas import tpu as pltpu

F32 = jnp.float32
BF16 = jnp.bfloat16

D_MODEL = 1024
HEAD_DIM = 64
LANES = 128
CHUNK = 64
RET_HEADS, SB_HEADS, GLA_HEADS = 6, 6, 4
RET_W, SB_W, GLA_W = RET_HEADS * HEAD_DIM, SB_HEADS * HEAD_DIM, GLA_HEADS * HEAD_DIM
GLA_GATE_RANK = 16
GLA_TAU = 16.0
D_FF = 2816
ROPE_BASE = 10000.0
EPS = 1e-6
IN_COLS = 4 * RET_W + 3 * SB_W + 4 * GLA_W + GLA_GATE_RANK
IN_PAD = 3840
RQ, RK, RV, RG = 0, RET_W, 2 * RET_W, 3 * RET_W
SQ, SK, SV = 4 * RET_W, 4 * RET_W + SB_W, 4 * RET_W + 2 * SB_W
GQ = 4 * RET_W + 3 * SB_W
GK, GV, GG = GQ + GLA_W, GQ + 2 * GLA_W, GQ + 3 * GLA_W
GLR = GQ + 4 * GLA_W
QK_SCALE = HEAD_DIM ** -0.5

RET_T = 256
GLA_T = 128
SB_TQ = 256
SB_TK = 256
FF_CK = 1408
VMEM_LIMIT = 56 * 1024 * 1024


def _dot(a, b):
    return jnp.dot(a, b, preferred_element_type=F32)


def _dot_nt(a, b):
    return lax.dot_general(a, b, (((1,), (1,)), ((), ())), preferred_element_type=F32)


def _dot_tn(a, b):
    return lax.dot_general(a, b, (((0,), (0,)), ((), ())), preferred_element_type=F32)


def _softplus(z):
    return jnp.maximum(z, 0.0) + jnp.log1p(jnp.exp(-jnp.abs(z)))


def _split_bf16(x):
    hi = x.astype(BF16)
    lo = (x - hi.astype(F32)).astype(BF16)
    return hi, lo


def _params(sem):
    return pltpu.CompilerParams(dimension_semantics=sem, vmem_limit_bytes=VMEM_LIMIT)


def _mod_kernel(c_ref, w_ref, b_ref, o_ref):
    c = c_ref[...]
    s = c * jax.nn.sigmoid(c)
    o_ref[0] = _dot(s.astype(BF16), w_ref[0].astype(BF16)) + b_ref[0]


def _modulation(c, ada_w, ada_b):
    depth, _, n6 = ada_w.shape
    b = c.shape[0]
    bp = 8
    tn = 1536
    c_pad = jnp.zeros((bp, D_MODEL), F32).at[:b].set(c)
    out = pl.pallas_call(
        _mod_kernel,
        grid=(depth, n6 // tn),
        in_specs=[
            pl.BlockSpec((bp, D_MODEL), lambda l, j: (0, 0)),
            pl.BlockSpec((1, D_MODEL, tn), lambda l, j: (l, 0, j)),
            pl.BlockSpec((1, 1, tn), lambda l, j: (l, 0, j)),
        ],
        out_specs=pl.BlockSpec((1, bp, tn), lambda l, j: (l, 0, j)),
        out_shape=jax.ShapeDtypeStruct((depth, bp, n6), F32),
        compiler_params=_params(("parallel", "parallel")),
        name="modulation",
    )(c_pad, ada_w, ada_b.reshape(depth, 1, n6))
    return out[:, :b].reshape(depth, b, 6, 1, D_MODEL).transpose(0, 2, 1, 3, 4)


def _rmsnorm_mod(x, g, sc, sh):
    ms = jnp.mean(x * x, axis=-1, keepdims=True)
    return (x * lax.rsqrt(ms + EPS) * g) * (1.0 + sc) + sh


def _inproj_kernel(x_ref, g_ref, sc_ref, sh_ref, cos_ref, sin_ref, w_ref, o_ref):
    tm = x_ref.shape[0]
    hb = _rmsnorm_mod(x_ref[...], g_ref[...], sc_ref[0], sh_ref[0]).astype(BF16)
    lane = lax.broadcasted_iota(jnp.int32, (tm, LANES), 1)
    first_half = (lane & 32) == 0
    cos_t = cos_ref[...]
    sin_t = sin_ref[...]

    def rotary(p):
        partner = jnp.where(first_half, pltpu.roll(p, LANES - 32, 1), pltpu.roll(p, 32, 1))
        return p * cos_t + partner * sin_t

    def silu(p):
        return p * jax.nn.sigmoid(p)

    def emit(start, width, fn):
        p = _dot(hb, w_ref[:, start:start + width])
        for j in range(width // LANES):
            blk = p[:, j * LANES:(j + 1) * LANES]
            o_ref[:, start + j * LANES:start + (j + 1) * LANES] = fn(blk).astype(BF16)

    ident = lambda p: p
    emit(RQ, RET_W, rotary)
    emit(RK, RET_W, lambda p: rotary(p) * QK_SCALE)
    emit(RV, RET_W, ident)
    emit(RG, RET_W, silu)
    emit(SQ, SB_W, lambda p: p * QK_SCALE)
    emit(SK, 2 * SB_W, ident)
    emit(GQ, GLA_W, lambda p: p * QK_SCALE)
    emit(GK, 2 * GLA_W, ident)
    emit(GG, GLA_W, silu)
    emit(GLR, IN_PAD - GLR, ident)


def _in_projection(x2, g, sc, sh, cos_t, sin_t, w_pad, seq):
    n = x2.shape[0]
    tm = min(512, seq)
    per_seq = seq // tm
    return pl.pallas_call(
        _inproj_kernel,
        grid=(n // tm,),
        in_specs=[
            pl.BlockSpec((tm, D_MODEL), lambda i: (i, 0)),
            pl.BlockSpec((1, D_MODEL), lambda i: (0, 0)),
            pl.BlockSpec((1, 1, D_MODEL), lambda i: (i // per_seq, 0, 0)),
            pl.BlockSpec((1, 1, D_MODEL), lambda i: (i // per_seq, 0, 0)),
            pl.BlockSpec((tm, LANES), lambda i: (i % per_seq, 0)),
            pl.BlockSpec((tm, LANES), lambda i: (i % per_seq, 0)),
            pl.BlockSpec((D_MODEL, IN_PAD), lambda i: (0, 0)),
        ],
        out_specs=pl.BlockSpec((tm, IN_PAD), lambda i: (i, 0)),
        out_shape=jax.ShapeDtypeStruct((n, IN_PAD), BF16),
        compiler_params=_params(("parallel",)),
        name="in_projection",
    )(x2, g, sc, sh, cos_t, sin_t, w_pad)


def _head_sums(x, m0):
    s0 = jnp.sum(jnp.where(m0, x, 0.0), axis=-1, keepdims=True)
    s1 = jnp.sum(jnp.where(m0, 0.0, x), axis=-1, keepdims=True)
    return jnp.where(m0, s0, s1)


def _ret_kernel(q_ref, k_ref, v_ref, gate_ref, gn_ref, dm_ref, qd_ref, kd_ref, gam_ref, bd_ref, o_ref,
                *, t, n_steps):
    lane = lax.broadcasted_iota(jnp.int32, (t, LANES), 1)
    m0 = lane < HEAD_DIM
    dm0 = dm_ref[0, 0]
    dm1 = dm_ref[0, 1]
    qd = qd_ref[0]
    kd = kd_ref[0]
    gam = gam_ref[0]
    bd = bd_ref[...]
    gn = gn_ref[...]
    zero = jnp.zeros((), BF16)

    def step(i, state):
        rows = pl.ds(pl.multiple_of(i * t, t), t)
        q = q_ref[rows, :]
        k = k_ref[rows, :]
        v = v_ref[rows, :]
        s0 = _dot_nt(jnp.where(m0, q, zero), k) * dm0
        s1 = _dot_nt(jnp.where(m0, zero, q), k) * dm1
        intra = jnp.where(m0, _dot(s0.astype(BF16), v), _dot(s1.astype(BF16), v))
        inter = _dot((q.astype(F32) * qd).astype(BF16), state.astype(BF16))
        tot = intra + inter
        mu = _head_sums(tot, m0) * (1.0 / HEAD_DIM)
        cen = tot - mu
        var = _head_sums(cen * cen, m0) * (1.0 / HEAD_DIM)
        y = cen * lax.rsqrt(var + EPS) * gn
        o_ref[rows, :] = (y * gate_ref[rows, :].astype(F32)).astype(BF16)
        kv = _dot_tn((k.astype(F32) * kd).astype(BF16), v)
        return gam * state + kv * bd

    lax.fori_loop(0, n_steps, step, jnp.zeros((LANES, LANES), F32))


def _retention_tables(t):
    hh = jnp.arange(RET_HEADS, dtype=F32)
    log_gamma = jnp.log1p(-(2.0 ** (-5.0 - hh)))
    pos = jnp.arange(t, dtype=F32)
    rel = pos[:, None] - pos[None, :]
    cn = jnp.arange(t)[:, None] // CHUNK
    cm = jnp.arange(t)[None, :] // CHUNK
    expo = jnp.where(cn == cm, jnp.abs(rel), rel)
    dm = jnp.where((cm <= cn)[None], jnp.exp(log_gamma[:, None, None] * expo[None]), 0.0)
    dm = dm.reshape(RET_HEADS // 2, 2, t, t)
    lg_lane = jnp.repeat(log_gamma, HEAD_DIM).reshape(RET_HEADS // 2, 1, LANES)
    qd = jnp.exp(lg_lane * (pos[None, :, None] + 1.0))
    kd = jnp.exp(lg_lane * (t - 1.0 - pos[None, :, None]))
    lane = jnp.arange(LANES)
    bd = ((lane[:, None] // HEAD_DIM) == (lane[None, :] // HEAD_DIM)).astype(F32)
    gam = jnp.exp(lg_lane.reshape(RET_HEADS // 2, LANES, 1) * t) * bd[None]
    return dm, qd, kd, gam, bd


def _retention(proj, gn, batch, seq):
    t = min(RET_T, seq)
    n_steps = seq // t
    dm, qd, kd, gam, bd = _retention_tables(t)
    col = lambda off: (lambda b, p: (b, off // LANES + p))
    seg = lambda off: pl.BlockSpec((seq, LANES), col(off))
    return pl.pallas_call(
        functools.partial(_ret_kernel, t=t, n_steps=n_steps),
        grid=(batch, RET_HEADS // 2),
        in_specs=[
            seg(RQ), seg(RK), seg(RV), seg(RG),
            pl.BlockSpec((1, LANES), lambda b, p: (0, p)),
            pl.BlockSpec((1, 2, t, t), lambda b, p: (p, 0, 0, 0)),
            pl.BlockSpec((1, t, LANES), lambda b, p: (p, 0, 0)),
            pl.BlockSpec((1, t, LANES), lambda b, p: (p, 0, 0)),
            pl.BlockSpec((1, LANES, LANES), lambda b, p: (p, 0, 0)),
            pl.BlockSpec((LANES, LANES), lambda b, p: (0, 0)),
        ],
        out_specs=pl.BlockSpec((seq, LANES), lambda b, p: (b, p)),
        out_shape=jax.ShapeDtypeStruct((batch * seq, RET_W), BF16),
        compiler_params=_params(("parallel", "parallel")),
        name="retention",
    )(proj, proj, proj, proj, gn, dm, qd, kd, gam, bd)


def _sb_kernel(q_ref, k_ref, v_ref, tri_ref, o_ref, *, tq, tk):
    i = pl.program_id(2)
    q = q_ref[...]
    tri = tri_ref[...]
    lane = lax.broadcasted_iota(jnp.int32, (tq, LANES), 1)
    m0 = lane < HEAD_DIM
    row = lax.broadcasted_iota(jnp.int32, (tq, tk), 0)
    col = lax.broadcasted_iota(jnp.int32, (tq, tk), 1)
    strict = col < row
    zero = jnp.zeros((), BF16)
    reps = tk // LANES

    def block(qh, j, carry, acc, masked):
        rows = pl.ds(pl.multiple_of(j * tk, tk), tk)
        z = _dot_nt(qh, k_ref[rows, :])
        sp = _softplus(z)
        lk = -sp
        if masked:
            lk = jnp.where(strict, lk, 0.0)
        hi, lo = _split_bf16(lk)
        cs = _dot(hi, tri) + _dot(lo, tri)
        between = cs[:, :tk] + jnp.concatenate([carry] * reps, axis=1)
        a = jnp.exp(z - sp + between)
        if masked:
            a = jnp.where(strict, a, 0.0)
        acc = acc + _dot(a.astype(BF16), v_ref[rows, :])
        return carry + cs[:, tk:], acc

    def head(qh):
        carry = jnp.zeros((tq, LANES), F32)
        acc = jnp.zeros((tq, LANES), F32)
        carry, acc = block(qh, i, carry, acc, True)

        def body(s, c):
            return block(qh, i - 1 - s, c[0], c[1], False)

        carry, acc = lax.fori_loop(0, i, body, (carry, acc))
        return acc

    o0 = head(jnp.where(m0, q, zero))
    o1 = head(jnp.where(m0, zero, q))
    o_ref[...] = jnp.where(m0, o0, o1).astype(BF16)


def _stick_breaking(proj, batch, seq):
    tq = min(SB_TQ, seq)
    tk = tq
    nq = seq // tq
    j = jnp.arange(tk)
    later = (j[:, None] > j[None, :]).astype(BF16)
    tri = jnp.concatenate([later, jnp.ones((tk, LANES), BF16)], axis=1)
    return pl.pallas_call(
        functools.partial(_sb_kernel, tq=tq, tk=tk),
        grid=(batch, SB_HEADS // 2, nq),
        in_specs=[
            pl.BlockSpec((tq, LANES), lambda b, p, i: (b * nq + i, SQ // LANES + p)),
            pl.BlockSpec((seq, LANES), lambda b, p, i: (b, SK // LANES + p)),
            pl.BlockSpec((seq, LANES), lambda b, p, i: (b, SV // LANES + p)),
            pl.BlockSpec((tk, tk + LANES), lambda b, p, i: (0, 0)),
        ],
        out_specs=pl.BlockSpec((tq, LANES), lambda b, p, i: (b * nq + i, p)),
        out_shape=jax.ShapeDtypeStruct((batch * seq, SB_W), BF16),
        compiler_params=_params(("parallel", "parallel", "arbitrary")),
        name="stick_breaking",
    )(proj, proj, proj, tri)


def _gla_kernel(q_ref, k_ref, v_ref, gate_ref, glr_ref, w2_ref, b_ref, gn_ref, tri_ref, bd_ref, o_ref,
                *, t, n_steps):
    lane = lax.broadcasted_iota(jnp.int32, (t, LANES), 1)
    m0 = lane < HEAD_DIM
    row = lax.broadcasted_iota(jnp.int32, (t, t), 0)
    col = lax.broadcasted_iota(jnp.int32, (t, t), 1)
    same = (row // CHUNK) == (col // CHUNK)
    use_lower = ((col // CHUNK) < (row // CHUNK)) | (same & (col <= row))
    use_upper = same & (col > row)
    tri = tri_ref[...]
    bd = bd_ref[...]
    w2 = w2_ref[...]
    bias = b_ref[...]
    gn = gn_ref[...]
    zero = jnp.zeros((), BF16)

    def step(i, st):
        rows = pl.ds(pl.multiple_of(i * t, t), t)
        q = q_ref[rows, :].astype(F32)
        k = k_ref[rows, :].astype(F32)
        v = v_ref[rows, :]
        z = _dot(glr_ref[rows, :], w2) + bias
        la = _softplus(-z) * (-1.0 / GLA_TAU)
        hi, lo = _split_bf16(la)
        bc = _dot(tri, hi) + _dot(tri, lo)
        bl = bc[t - 1:t, :]
        eb = jnp.exp(bc)
        enb = jnp.exp(-bc)
        qt = (q * eb).astype(BF16)
        kt = (k * enb).astype(BF16)
        qu = (q * enb).astype(BF16)
        ku = (k * eb).astype(BF16)

        def head_out(mh_first):
            sel = (lambda a: jnp.where(m0, a, zero)) if mh_first else (lambda a: jnp.where(m0, zero, a))
            scl = _dot_nt(sel(qt), kt)
            scu = _dot_nt(sel(qu), ku)
            sc = jnp.where(use_lower, scl, jnp.where(use_upper, scu, 0.0))
            return _dot(sc.astype(BF16), v)

        intra = jnp.where(m0, head_out(True), head_out(False))
        inter = _dot_nt(qt, st.astype(BF16))
        tot = intra + inter
        ms = _head_sums(tot * tot, m0) * (1.0 / HEAD_DIM)
        y = tot * lax.rsqrt(ms + EPS) * gn
        o_ref[rows, :] = (y * gate_ref[rows, :].astype(F32)).astype(BF16)
        kvt = _dot_tn(v, (k * jnp.exp(bl - bc)).astype(BF16))
        return st * jnp.exp(bl) + kvt * bd

    lax.fori_loop(0, n_steps, step, jnp.zeros((LANES, LANES), F32))


def _gla(proj, w2p, bias, gn, batch, seq):
    t = min(GLA_T, seq)
    n_steps = seq // t
    j = jnp.arange(t)
    tri = (j[None, :] <= j[:, None]).astype(BF16)
    lane = jnp.arange(LANES)
    bd = ((lane[:, None] // HEAD_DIM) == (lane[None, :] // HEAD_DIM)).astype(F32)
    col = lambda off: (lambda b, p: (b, off // LANES + p))
    seg = lambda off: pl.BlockSpec((seq, LANES), col(off))
    return pl.pallas_call(
        functools.partial(_gla_kernel, t=t, n_steps=n_steps),
        grid=(batch, GLA_HEADS // 2),
        in_specs=[
            seg(GQ), seg(GK), seg(GV), seg(GG),
            pl.BlockSpec((seq, LANES), lambda b, p: (b, GLR // LANES)),
            pl.BlockSpec((LANES, LANES), lambda b, p: (0, p)),
            pl.BlockSpec((1, LANES), lambda b, p: (0, p)),
            pl.BlockSpec((1, LANES), lambda b, p: (0, p)),
            pl.BlockSpec((t, t), lambda b, p: (0, 0)),
            pl.BlockSpec((LANES, LANES), lambda b, p: (0, 0)),
        ],
        out_specs=pl.BlockSpec((seq, LANES), lambda b, p: (b, p)),
        out_shape=jax.ShapeDtypeStruct((batch * seq, GLA_W), BF16),
        compiler_params=_params(("parallel", "parallel")),
        name="gla",
    )(proj, proj, proj, proj, proj, w2p, bias, gn, tri, bd)


def _rmsnorm(y, g):
    return y * lax.rsqrt(jnp.mean(y * y, axis=-1, keepdims=True) + EPS) * g


def _outproj_kernel(x_ref, ret_ref, sb_ref, go_ref, w_ref, g_ref, gate_ref, o_ref):
    y = _dot(ret_ref[...], w_ref[0:RET_W, :])
    y = y + _dot(sb_ref[...], w_ref[RET_W:RET_W + SB_W, :])
    y = y + _dot(go_ref[...], w_ref[RET_W + SB_W:D_MODEL, :])
    o_ref[...] = x_ref[...] + gate_ref[0] * _rmsnorm(y, g_ref[...])


def _out_projection(x2, ret, sb, go, w_out, g, gate, seq):
    n = x2.shape[0]
    tm = min(512, seq)
    per_seq = seq // tm
    row = lambda w: pl.BlockSpec((tm, w), lambda i: (i, 0))
    return pl.pallas_call(
        _outproj_kernel,
        grid=(n // tm,),
        in_specs=[
            row(D_MODEL), row(RET_W), row(SB_W), row(GLA_W),
            pl.BlockSpec((D_MODEL, D_MODEL), lambda i: (0, 0)),
            pl.BlockSpec((1, D_MODEL), lambda i: (0, 0)),
            pl.BlockSpec((1, 1, D_MODEL), lambda i: (i // per_seq, 0, 0)),
        ],
        out_specs=row(D_MODEL),
        out_shape=jax.ShapeDtypeStruct((n, D_MODEL), F32),
        compiler_params=_params(("parallel",)),
        name="out_projection",
    )(x2, ret, sb, go, w_out, g, gate)


HALO = 8


def _gelu_tanh(a):
    return 0.5 * a * (1.0 + jnp.tanh(0.7978845608028654 * (a + 0.044715 * (a * a * a))))


def _ffn_kernel(x_ref, g_ref, sc_ref, sh_ref, wu_ref, cw_ref, cb_ref, wd_ref, pg_ref, gate_ref, o_ref,
                abuf_ref, tail_ref, *, per_seq):
    tm = x_ref.shape[0]
    i = pl.program_id(0)
    x = x_ref[...]
    hb = _rmsnorm_mod(x, g_ref[...], sc_ref[0], sh_ref[0]).astype(BF16)

    @pl.when(i % per_seq == 0)
    def _():
        tail_ref[...] = jnp.zeros_like(tail_ref)

    y = jnp.zeros((tm, D_MODEL), F32)
    for c in range(D_FF // FF_CK):
        c0 = c * FF_CK
        a = _dot(hb, wu_ref[:, c0:c0 + FF_CK])
        val = _dot(hb, wu_ref[:, D_FF + c0:D_FF + c0 + FF_CK])
        abuf_ref[0:HALO, :] = tail_ref[c]
        abuf_ref[HALO:HALO + tm, :] = a
        tail_ref[c] = a[tm - HALO:tm, :]
        conv = (a * cw_ref[2:3, c0:c0 + FF_CK]
                + abuf_ref[HALO - 1:HALO - 1 + tm, :] * cw_ref[1:2, c0:c0 + FF_CK]
                + abuf_ref[HALO - 2:HALO - 2 + tm, :] * cw_ref[0:1, c0:c0 + FF_CK]
                + cb_ref[:, c0:c0 + FF_CK])
        h = (_gelu_tanh(conv) * val).astype(BF16)
        y = y + _dot(h, wd_ref[c0:c0 + FF_CK, :])
    o_ref[...] = x + gate_ref[0] * _rmsnorm(y, pg_ref[...])


def _conv_ffn(x2, g, sc, sh, w_up, conv_w, conv_b, w_down, pg, gate, seq):
    n = x2.shape[0]
    tm = min(512, seq)
    per_seq = seq // tm
    const = lambda shape: pl.BlockSpec(shape, lambda i: (0,) * len(shape))
    mod = pl.BlockSpec((1, 1, D_MODEL), lambda i: (i // per_seq, 0, 0))
    return pl.pallas_call(
        functools.partial(_ffn_kernel, per_seq=per_seq),
        grid=(n // tm,),
        in_specs=[
            pl.BlockSpec((tm, D_MODEL), lambda i: (i, 0)),
            const((1, D_MODEL)), mod, mod,
            const((D_MODEL, 2 * D_FF)), const((3, D_FF)), const((1, D_FF)), const((D_FF, D_MODEL)),
            const((1, D_MODEL)), mod,
        ],
        out_specs=pl.BlockSpec((tm, D_MODEL), lambda i: (i, 0)),
        out_shape=jax.ShapeDtypeStruct((n, D_MODEL), F32),
        scratch_shapes=[
            pltpu.VMEM((tm + HALO, FF_CK), F32),
            pltpu.VMEM((D_FF // FF_CK, HALO, FF_CK), F32),
        ],
        compiler_params=_params(("arbitrary",)),
        name="conv_ffn",
    )(x2, g, sc, sh, w_up, conv_w, conv_b, w_down, pg, gate)


def _rotary_tables(seq):
    half = HEAD_DIM // 2
    pos = jnp.arange(seq, dtype=F32)
    freqs = ROPE_BASE ** (-jnp.arange(half, dtype=F32) / half)
    ang = pos[:, None] * freqs[None, :]
    cos, sin = jnp.cos(ang), jnp.sin(ang)
    cos_t = jnp.tile(cos, (1, LANES // half))
    sin_t = jnp.tile(jnp.concatenate([-sin, sin], axis=1), (1, LANES // HEAD_DIM))
    return cos_t, sin_t


def kernel(x, c, ada_w, ada_b, pre_mix_g, post_mix_g, w_in, gla_w2, gla_b, ret_norm_g, gla_norm_g, w_out,
           pre_ffn_g, post_ffn_g, w_up, conv_w, conv_b, w_down):
    batch, seq, _ = x.shape
    depth = w_in.shape[0]
    mod = _modulation(c, ada_w, ada_b)
    cos_t, sin_t = _rotary_tables(seq)
    x2 = x.reshape(batch * seq, D_MODEL)
    row = lambda a: a.reshape(1, -1)
    for l in range(depth):
        sh1, sc1, g1, sh2, sc2, g2 = (mod[l, j] for j in range(6))
        w_pad = jnp.zeros((D_MODEL, IN_PAD), BF16).at[:, :IN_COLS].set(w_in[l].astype(BF16))
        w2p = jnp.zeros((LANES, GLA_W), BF16).at[:GLA_GATE_RANK].set(gla_w2[l].astype(BF16))
        proj = _in_projection(x2, row(pre_mix_g[l]), sc1, sh1, cos_t, sin_t, w_pad, seq)
        ret = _retention(proj, row(ret_norm_g[l]), batch, seq)
        sb = _stick_breaking(proj, batch, seq)
        go = _gla(proj, w2p, row(gla_b[l]), row(gla_norm_g[l]), batch, seq)
        x2 = _out_projection(x2, ret, sb, go, w_out[l].astype(BF16), row(post_mix_g[l]), g1, seq)
        x2 = _conv_ffn(x2, row(pre_ffn_g[l]), sc2, sh2, w_up[l].astype(BF16), conv_w[l], row(conv_b[l]),
                       w_down[l].astype(BF16), row(post_ffn_g[l]), g2, seq)
    return x2.reshape(batch, seq, D_MODEL)
```
